```python
import jax, jax.numpy as jnp
from jax import lax
import numpy as np

D_MODEL = 1024
BATCH = 8
SEQ = 2048
DEPTH = 4

PLE_DIM = 256
NORM_EPS = 1e-6

ATTN_HEADS = 4
ATTN_HEAD_DIM = 64
ATTN_WIDTH = ATTN_HEADS * ATTN_HEAD_DIM
MOBA_BLOCK = 256
MOBA_TOPK = 3
MOBA_QCHUNK = 64

RET_HEADS = 4
RET_KEY_DIM = 64
RET_VALUE_DIM = 128
RET_QK_WIDTH = RET_HEADS * RET_KEY_DIM
RET_V_WIDTH = RET_HEADS * RET_VALUE_DIM
RET_CHUNK = 128
ROPE_BASE = 10000.0

POOL_WINDOWS = (2, 4, 8, 16)
POOL_GROUPS = len(POOL_WINDOWS)
POOL_WIDTH = D_MODEL // 4
POOL_GROUP_DIM = POOL_WIDTH // POOL_GROUPS

N_BRANCHES = 3
IN_SPLIT_WIDTHS = (ATTN_WIDTH, ATTN_WIDTH, ATTN_WIDTH,
                   RET_QK_WIDTH, RET_QK_WIDTH, RET_V_WIDTH, RET_V_WIDTH,
                   POOL_WIDTH,
                   D_MODEL, D_MODEL, D_MODEL)
IN_WIDTH = sum(IN_SPLIT_WIDTHS)
IN_SPLIT_POINTS = tuple(int(s) for s in np.cumsum(IN_SPLIT_WIDTHS)[:-1])

FFN_DIM = 2816
CONV_WIDTH = 3

kernel_name = "hybrid_moba_pool_retention_trunk"


def rms_norm(x, g):
    xf = x.astype(jnp.float32)
    y = xf * lax.rsqrt(jnp.mean(xf * xf, axis=-1, keepdims=True) + NORM_EPS)
    return (y * g.astype(jnp.float32)).astype(x.dtype)


def moba_attention(q, k, v):
    B, T, H, dh = q.shape
    dt = q.dtype
    n_blk = -(-T // MOBA_BLOCK)
    t_pad = n_blk * MOBA_BLOCK
    pad = ((0, 0), (0, t_pad - T), (0, 0), (0, 0))
    q, k, v = jnp.pad(q, pad), jnp.pad(k, pad), jnp.pad(v, pad)
    kb = k.reshape(B, n_blk, MOBA_BLOCK, H, dh).transpose(0, 3, 1, 2, 4)
    vb = v.reshape(B, n_blk, MOBA_BLOCK, H, dh).transpose(0, 3, 1, 2, 4)
    k_mean = jnp.mean(kb.astype(jnp.float32), axis=3)
    n_chunks = t_pad // MOBA_QCHUNK
    qc = q.reshape(B, n_chunks, MOBA_QCHUNK, H, dh).transpose(1, 0, 3, 2, 4)
    top_k = min(MOBA_TOPK, n_blk)
    scale = ATTN_HEAD_DIM ** -0.5
    blk_ids = jnp.arange(n_blk)
    b_idx = jnp.arange(B)[:, None, None, None]
    h_idx = jnp.arange(H)[None, :, None, None]
    q_off = jnp.arange(MOBA_QCHUNK)
    k_off = jnp.arange(MOBA_BLOCK)

    def chunk_fn(args):
        q_c, c = args
        start = c * MOBA_QCHUNK
        own = start // MOBA_BLOCK
        gate = jnp.einsum('bhqd,bhnd->bhqn', q_c.astype(jnp.float32), k_mean)
        gate = jnp.where(blk_ids[None, None, None, :] < own, gate, -jnp.inf)
        _, sel = lax.top_k(gate, top_k)
        valid = sel < own
        k_sel = kb[b_idx, h_idx, sel]
        v_sel = vb[b_idx, h_idx, sel]
        s_sel = jnp.einsum('bhqd,bhqjld->bhqjl', q_c, k_sel).astype(jnp.float32) * scale
        s_sel = jnp.where(valid[..., None], s_sel, -jnp.inf)
        s_sel = s_sel.reshape(B, H, MOBA_QCHUNK, top_k * MOBA_BLOCK)
        k_own = lax.dynamic_index_in_dim(kb, own, axis=2, keepdims=False)
        v_own = lax.dynamic_index_in_dim(vb, own, axis=2, keepdims=False)
        s_own = jnp.einsum('bhqd,bhld->bhql', q_c, k_own).astype(jnp.float32) * scale
        causal = (own * MOBA_BLOCK + k_off)[None, :] <= (start + q_off)[:, None]
        s_own = jnp.where(causal[None, None], s_own, -jnp.inf)
        probs = jax.nn.softmax(jnp.concatenate([s_sel, s_own], axis=-1), axis=-1).astype(dt)
        p_sel = probs[..., :top_k * MOBA_BLOCK].reshape(B, H, MOBA_QCHUNK, top_k, MOBA_BLOCK)
        p_own = probs[..., top_k * MOBA_BLOCK:]
        return (jnp.einsum('bhqjl,bhqjld->bhqd', p_sel, v_sel)
                + jnp.einsum('bhql,bhld->bhqd', p_own, v_own))

    out = lax.map(chunk_fn, (qc, jnp.arange(n_chunks, dtype=jnp.int32)))
    out = out.transpose(1, 0, 3, 2, 4).reshape(B, t_pad, H * dh)
    return out[:, :T].astype(dt)


def rotary(x, pos):
    half = x.shape[-1] // 2
    inv_freq = ROPE_BASE ** (-jnp.arange(half, dtype=jnp.float32) / half)
    ang = pos[:, None] * inv_freq[None, :]
    cos = jnp.cos(ang)[None, :, None, :]
    sin = jnp.sin(ang)[None, :, None, :]
    x1, x2 = x[..., :half], x[..., half:]
    return jnp.concatenate([x1 * cos - x2 * sin, x1 * sin + x2 * cos], axis=-1)


def retention(q, k, v, g):
    B, T, H, dk = q.shape
    dv = v.shape[-1]
    dt = v.dtype
    f32 = jnp.float32
    pos = jnp.arange(T, dtype=f32)
    q = rotary(q.astype(f32), pos)
    k = rotary(k.astype(f32), pos) * (dk ** -0.5)
    v = v.astype(f32)
    log_gamma = jnp.log1p(-jnp.exp2(-5.0 - jnp.arange(H, dtype=f32)))
    C = RET_CHUNK
    nc = T // C
    qc = q.reshape(B, nc, C, H, dk)
    kc = k.reshape(B, nc, C, H, dk)
    vc = v.reshape(B, nc, C, H, dv)
    idx = jnp.arange(C, dtype=f32)
    rel = idx[:, None] - idx[None, :]
    decay = jnp.where(rel[None] >= 0, jnp.exp(jnp.maximum(rel, 0.0)[None] * log_gamma[:, None, None]), 0.0)
    scores = jnp.einsum('bnihd,bnjhd->bnhij', qc, kc) * decay[None, None]
    inner = jnp.einsum('bnhij,bnjhe->bnihe', scores, vc)
    zeta = jnp.exp((C - 1.0 - idx)[None, :] * log_gamma[:, None])
    xi = jnp.exp((idx + 1.0)[None, :] * log_gamma[:, None])
    gamma_chunk = jnp.exp(C * log_gamma)
    u = jnp.einsum('bnjhd,bnjhe,hj->bnhde', kc, vc, zeta)

    def step(state, u_n):
        return u_n + gamma_chunk[None, :, None, None] * state, state

    _, r_prev = lax.scan(step, jnp.zeros((B, H, dk, dv), f32), jnp.moveaxis(u, 1, 0))
    r_prev = jnp.moveaxis(r_prev, 0, 1)
    cross = jnp.einsum('bnihd,bnhde,hi->bnihe', qc, r_prev, xi)
    y = (inner + cross).reshape(B, T, H, dv)
    mu = jnp.mean(y, axis=-1, keepdims=True)
    var = jnp.mean(jnp.square(y - mu), axis=-1, keepdims=True)
    y = ((y - mu) * lax.rsqrt(var + NORM_EPS)).reshape(B, T, H * dv)
    return (jax.nn.silu(g.astype(f32)) * y).astype(dt)


def multiscale_pool(u, w_group, scale):
    B, T, _ = u.shape
    f32 = jnp.float32
    uf = u.astype(f32).reshape(B, T, POOL_GROUPS, POOL_GROUP_DIM)
    cs = jnp.cumsum(uf, axis=1)
    cs = jnp.concatenate([jnp.zeros_like(cs[:, :1]), cs], axis=1)
    t = jnp.arange(T)
    win = jnp.array(POOL_WINDOWS, dtype=jnp.int32)
    lo = jnp.maximum(t[:, None] + 1 - win[None, :], 0)
    g_ids = jnp.arange(POOL_GROUPS)[None, :]
    window_sum = cs[:, 1:] - cs[:, lo, g_ids]
    count = (t[:, None] + 1 - lo).astype(f32)
    mixed = window_sum / count[None, :, :, None] - uf
    y = jnp.einsum('btgc,gcd->btgd', mixed, w_group.astype(f32)).reshape(B, T, POOL_WIDTH)
    return (y * scale.astype(f32)).astype(u.dtype)


def causal_dwconv(u, w, b):
    C = u.shape[-1]
    y = lax.conv_general_dilated(u, w.astype(u.dtype)[:, None, :], window_strides=(1,),
                                 padding=[(CONV_WIDTH - 1, 0)],
                                 dimension_numbers=('NWC', 'WIO', 'NWC'),
                                 feature_group_count=C)
    return y + b.astype(u.dtype)


def setup_inputs(seed: int = 0) -> dict:
    key = jax.random.key(seed)
    ks = jax.random.split(key, 20)
    f32 = jnp.float32

    def w(k, shape, fan_in):
        return jax.random.normal(k, shape, f32) * (fan_in ** -0.5)

    def gain(k, shape):
        return 1.0 + 0.05 * jax.random.normal(k, shape, f32)

    return {
        "x": jax.random.normal(ks[0], (BATCH, SEQ, D_MODEL), f32),
        "p": jax.random.normal(ks[1], (DEPTH, BATCH, SEQ, PLE_DIM), f32),
        "norm_mix_g": gain(ks[2], (DEPTH, D_MODEL)),
        "w_in": w(ks[3], (DEPTH, D_MODEL, IN_WIDTH), D_MODEL),
        "w_branch_attn": w(ks[4], (DEPTH, ATTN_WIDTH, D_MODEL), ATTN_WIDTH),
        "w_branch_ret": w(ks[5], (DEPTH, RET_V_WIDTH, D_MODEL), RET_V_WIDTH),
        "w_branch_pool": w(ks[6], (DEPTH, POOL_WIDTH, D_MODEL), POOL_WIDTH),
        "pool_w": w(ks[7], (DEPTH, POOL_GROUPS, POOL_GROUP_DIM, POOL_GROUP_DIM), POOL_GROUP_DIM),
        "pool_scale": gain(ks[8], (DEPTH, POOL_WIDTH)),
        "w_out": w(ks[9], (DEPTH, D_MODEL, D_MODEL), D_MODEL),
        "norm_ffn_g": gain(ks[10], (DEPTH, D_MODEL)),
        "w_up": w(ks[11], (DEPTH, D_MODEL, 2 * FFN_DIM), D_MODEL),
        "conv_w": w(ks[12], (DEPTH, CONV_WIDTH, 2 * FFN_DIM), CONV_WIDTH),
        "conv_b": 0.02 * jax.random.normal(ks[13], (DEPTH, 2 * FFN_DIM), f32),
        "w_down": w(ks[14], (DEPTH, FFN_DIM, D_MODEL), FFN_DIM),
        "norm_ple_g": gain(ks[15], (DEPTH, D_MODEL)),
        "w_ple_gate": w(ks[16], (DEPTH, D_MODEL, D_MODEL), D_MODEL),
        "w_ple_proj": w(ks[17], (DEPTH, PLE_DIM, D_MODEL), PLE_DIM),
        "norm_final_g": gain(ks[18], (D_MODEL,)),
    }


def reference(x, p, norm_mix_g, w_in, w_branch_attn, w_branch_ret, w_branch_pool, pool_w,
              pool_scale, w_out, norm_ffn_g, w_up, conv_w, conv_b, w_down, norm_ple_g,
              w_ple_gate, w_ple_proj, norm_final_g):
    B, T, _ = x.shape
    for i in range(DEPTH):
        h = rms_norm(x, norm_mix_g[i])
        proj = h @ w_in[i]
        (qa, ka, va, qr, kr, vr, gr, u_pool, gate_a, gate_r, gate_p) = jnp.split(proj, IN_SPLIT_POINTS, axis=-1)
        attn = moba_attention(qa.reshape(B, T, ATTN_HEADS, ATTN_HEAD_DIM),
                              ka.reshape(B, T, ATTN_HEADS, ATTN_HEAD_DIM),
                              va.reshape(B, T, ATTN_HEADS, ATTN_HEAD_DIM))
        ret = retention(qr.reshape(B, T, RET_HEADS, RET_KEY_DIM),
                        kr.reshape(B, T, RET_HEADS, RET_KEY_DIM),
                        vr.reshape(B, T, RET_HEADS, RET_VALUE_DIM), gr)
        pooled = multiscale_pool(u_pool, pool_w[i], pool_scale[i])
        merged = (jax.nn.sigmoid(gate_a) * (attn @ w_branch_attn[i])
                  + jax.nn.sigmoid(gate_r) * (ret @ w_branch_ret[i])
                  + jax.nn.sigmoid(gate_p) * (pooled @ w_branch_pool[i]))
        x = x + merged @ w_out[i]
        h = rms_norm(x, norm_ffn_g[i])
        up = causal_dwconv(h @ w_up[i], conv_w[i], conv_b[i])
        a, b = jnp.split(up, 2, axis=-1)
        x = x + (jax.nn.gelu(a) * b) @ w_down[i]
        h = rms_norm(x, norm_ple_g[i])
        x = x + jax.nn.sigmoid(h @ w_ple_gate[i]) * (p[i] @ w_ple_proj[i])
    return rms_norm(x, norm_final_g)
```

```python
import functools

import jax
import jax.numpy as jnp
import numpy as np
from jax import lax
from jax.experimental import pallas as pl
from jax.experimental.pallas import tpu as pltpu

F32 = jnp.float32
BF16 = jnp.bfloat16

NORM_EPS = 1e-6

ATTN_HEADS = 4
ATTN_HEAD_DIM = 64
ATTN_WIDTH = ATTN_HEADS * ATTN_HEAD_DIM
MOBA_BLOCK = 256
MOBA_TOPK = 3

RET_HEADS = 4
RET_KEY_DIM = 64
RET_VALUE_DIM = 128
RET_QK_WIDTH = RET_HEADS * RET_KEY_DIM
RET_V_WIDTH = RET_HEADS * RET_VALUE_DIM
ROPE_BASE = 10000.0
RET_KERNEL_CHUNK = 256

POOL_WINDOWS = (2, 4, 8, 16)
POOL_GROUPS = len(POOL_WINDOWS)
POOL_PAD_ROWS = 16

CONV_WIDTH = 3
CONV_HALO_ROWS = 8

LANE_TILE = 128
TOKEN_TILE = 512
FFN_COL_TILE = 256
VMEM_LIMIT_BYTES = 56 * 1024 * 1024


def _rms(x, g):
    return x * lax.rsqrt(jnp.mean(x * x, axis=-1, keepdims=True) + NORM_EPS) * g


def _dot(a, b):
    return jnp.dot(a, b, preferred_element_type=F32)


def _dot_nt(a, b):
    return lax.dot_general(a, b, (((1,), (1,)), ((), ())), preferred_element_type=F32)


def _resident(shape):
    nd = len(shape)
    return pl.BlockSpec(shape, lambda *_: (0,) * nd, pipeline_mode=pl.Buffered(1))


def _params(n_axes):
    return pltpu.CompilerParams(dimension_semantics=("arbitrary",) * n_axes,
                                vmem_limit_bytes=VMEM_LIMIT_BYTES)


def _inproj_kernel(x_ref, g_ref, w_ref, qkva_ref, qkr_ref, vr_ref, gr_ref, up_ref, gates_ref, *, splits):
    h = _rms(x_ref[...], g_ref[...]).astype(BF16)

    def proj(lo, hi):
        return _dot(h, w_ref[:, lo:hi])

    a0, r0, v0, g0, p0, s0, end = splits
    qkva_ref[...] = proj(a0, r0).astype(BF16)
    qkr_ref[...] = proj(r0, v0)
    vr_ref[...] = proj(v0, g0).astype(BF16)
    gr_ref[...] = proj(g0, p0)
    up_ref[...] = proj(p0, s0)
    d = x_ref.shape[-1]
    for c in range((end - s0) // d):
        gates_ref[:, c * d:(c + 1) * d] = jax.nn.sigmoid(proj(s0 + c * d, s0 + (c + 1) * d)).astype(BF16)


def _inproj(x, g, w):
    B, T, D = x.shape
    tm = min(TOKEN_TILE, T)
    aw, rq, rv, pw = 3 * ATTN_WIDTH, 2 * RET_QK_WIDTH, RET_V_WIDTH, D // 4
    splits = tuple(int(s) for s in np.cumsum([0, aw, rq, rv, rv, pw, 3 * D]))
    assert splits[-1] == w.shape[1]
    widths = (aw, rq, rv, rv, pw, 3 * D)
    dtypes = (BF16, F32, BF16, F32, F32, BF16)
    tile = lambda n: pl.BlockSpec((None, tm, n), lambda b, t: (b, t, 0))
    return pl.pallas_call(
        functools.partial(_inproj_kernel, splits=splits),
        grid=(B, T // tm),
        in_specs=[tile(D), _resident((1, D)), _resident(w.shape)],
        out_specs=[tile(n) for n in widths],
        out_shape=[jax.ShapeDtypeStruct((B, T, n), dt) for n, dt in zip(widths, dtypes)],
        compiler_params=_params(2),
        name="inproj",
    )(x, g, w)


def _moba_kernel(qkv_ref, o_ref, *, block, heads, head_dim, top_k):
    T = qkv_ref.shape[0]
    W = heads * head_dim
    L = block
    n_blk = T // L
    scale = head_dim ** -0.5
    lane = lax.broadcasted_iota(jnp.int32, (1, W), 1)
    head_masks = [(lane >= h * head_dim) & (lane < (h + 1) * head_dim) for h in range(heads)]

    k_all = qkv_ref[:, W:2 * W]
    v_all = qkv_ref[:, 2 * W:3 * W]
    v_masked = [jnp.where(m, v_all, jnp.zeros_like(v_all)) for m in head_masks]

    rows = lax.broadcasted_iota(jnp.int32, (LANE_TILE, T), 0)
    cols = lax.broadcasted_iota(jnp.int32, (LANE_TILE, T), 1)
    avg = jnp.where((cols >= rows * L) & (cols < (rows + 1) * L), 1.0 / L, 0.0).astype(BF16)
    k_mean = _dot(avg, k_all)
    k_mean_hi = k_mean.astype(BF16)
    k_mean_lo = (k_mean - k_mean_hi.astype(F32)).astype(BF16)

    blk_lane = lax.broadcasted_iota(jnp.int32, (L, LANE_TILE), 1)
    q_pos = lax.broadcasted_iota(jnp.int32, (L, L), 0)
    k_pos = lax.broadcasted_iota(jnp.int32, (L, L), 1)
    causal = k_pos <= q_pos
    neg_inf = jnp.float32(-jnp.inf)

    for i in range(n_blk):
        q_i = qkv_ref[i * L:(i + 1) * L, 0:W]
        acc = jnp.zeros((L, W), F32)
        for h in range(heads):
            q_h = jnp.where(head_masks[h], q_i, jnp.zeros_like(q_i))
            s_all = _dot_nt(q_h, k_all[0:(i + 1) * L, :]) * scale
            pieces = []
            if i > top_k:
                gate = _dot_nt(q_h, k_mean_hi) + _dot_nt(q_h, k_mean_lo)
                rank = jnp.zeros((L, LANE_TILE), jnp.int32)
                for j in range(i):
                    g_j = gate[:, j:j + 1]
                    beats = (g_j > gate) | ((g_j == gate) & (j < blk_lane))
                    rank = rank + beats.astype(jnp.int32)
                bias = jnp.where((rank < top_k) & (blk_lane < i), 0.0, neg_inf)
                for j in range(i):
                    pieces.append(s_all[:, j * L:(j + 1) * L] + bias[:, j:j + 1])
            else:
                for j in range(i):
                    pieces.append(s_all[:, j * L:(j + 1) * L])
            pieces.append(jnp.where(causal, s_all[:, i * L:(i + 1) * L], neg_inf))
            s = jnp.concatenate(pieces, axis=1) if len(pieces) > 1 else pieces[0]
            m = jnp.max(s, axis=-1, keepdims=True)
            p = jnp.exp(s - m)
            inv_l = 1.0 / jnp.sum(p, axis=-1, keepdims=True)
            acc = acc + _dot(p.astype(BF16), v_masked[h][0:(i + 1) * L, :]) * inv_l
        o_ref[i * L:(i + 1) * L, :] = acc.astype(o_ref.dtype)


def _moba(qkv):
    B, T, W3 = qkv.shape
    W = W3 // 3
    assert T % MOBA_BLOCK == 0
    return pl.pallas_call(
        functools.partial(_moba_kernel, block=MOBA_BLOCK, heads=ATTN_HEADS,
                          head_dim=ATTN_HEAD_DIM, top_k=MOBA_TOPK),
        grid=(B,),
        in_specs=[pl.BlockSpec((None, T, W3), lambda b: (b, 0, 0))],
        out_specs=pl.BlockSpec((None, T, W), lambda b: (b, 0, 0)),
        out_shape=jax.ShapeDtypeStruct((B, T, W), BF16),
        compiler_params=_params(1),
        name="moba",
    )(qkv)


def _retention_kernel(qk_ref, v_ref, g_ref, cos_ref, sin_ref, xi_ref, zeta_ref, decay_ref,
                      smask_ref, sgamma_ref, o_ref, state_ref, *, chunk, heads, dk, dv):
    T = qk_ref.shape[0]
    C = chunk
    Wq = heads * dk
    half = dk // 2
    lane = lax.broadcasted_iota(jnp.int32, (1, Wq), 1)
    first_half = (lane % dk) < half
    head_masks = [(lane >= h * dk) & (lane < (h + 1) * dk) for h in range(heads)]

    def rotate(x, c, s):
        partner = jnp.where(first_half, pltpu.roll(x, Wq - half, 1), pltpu.roll(x, half, 1))
        return x * c + partner * s

    state_ref[...] = jnp.zeros_like(state_ref)

    def body(n, carry):
        r = pl.ds(pl.multiple_of(n * C, C), C)
        c, s = cos_ref[r, :], sin_ref[r, :]
        q = rotate(qk_ref[r, 0:Wq], c, s)
        k = rotate(qk_ref[r, Wq:2 * Wq], c, s) * (dk ** -0.5)
        v = v_ref[r, :]
        k_b = k.astype(BF16)
        state = state_ref[...]
        cross = _dot((q * xi_ref[...]).astype(BF16), state.astype(BF16))
        outs = []
        for h in range(heads):
            q_h = jnp.where(head_masks[h], q, 0.0).astype(BF16)
            scores = _dot_nt(q_h, k_b) * decay_ref[h]
            y = _dot(scores.astype(BF16), v[:, h * dv:(h + 1) * dv]) + cross[:, h * dv:(h + 1) * dv]
            mu = jnp.mean(y, axis=-1, keepdims=True)
            yc = y - mu
            var = jnp.mean(yc * yc, axis=-1, keepdims=True)
            outs.append(yc * lax.rsqrt(var + NORM_EPS))
        y = jnp.concatenate(outs, axis=1)
        o_ref[r, :] = (jax.nn.silu(g_ref[r, :]) * y).astype(o_ref.dtype)
        u = _dot((k * zeta_ref[...]).T.astype(BF16), v)
        state_ref[...] = u * smask_ref[...] + state * sgamma_ref[...]
        return carry

    lax.fori_loop(0, T // C, body, 0)


def _retention_tables(T, C):
    H, dk, dv = RET_HEADS, RET_KEY_DIM, RET_VALUE_DIM
    half = dk // 2
    pos = jnp.arange(T, dtype=F32)
    inv_freq = ROPE_BASE ** (-jnp.arange(half, dtype=F32) / half)
    ang = pos[:, None] * inv_freq[None, :]
    cos, sin = jnp.cos(ang), jnp.sin(ang)
    cos_t = jnp.tile(jnp.concatenate([cos, cos], axis=1), (1, H))
    sin_t = jnp.tile(jnp.concatenate([-sin, sin], axis=1), (1, H))
    log_gamma = jnp.log1p(-jnp.exp2(-5.0 - jnp.arange(H, dtype=F32)))
    idx = jnp.arange(C, dtype=F32)
    rel = idx[:, None] - idx[None, :]
    decay = jnp.where(rel[None] >= 0, jnp.exp(jnp.maximum(rel, 0.0)[None] * log_gamma[:, None, None]), 0.0)
    zeta = jnp.exp((C - 1.0 - idx)[None, :] * log_gamma[:, None])
    xi = jnp.exp((idx + 1.0)[None, :] * log_gamma[:, None])
    gamma_chunk = jnp.exp(C * log_gamma)
    xi_t = jnp.repeat(xi.T, dk, axis=1)
    zeta_t = jnp.repeat(zeta.T, dk, axis=1)
    row_head = jnp.arange(H * dk) // dk
    col_head = jnp.arange(H * dv) // dv
    smask = (row_head[:, None] == col_head[None, :]).astype(F32)
    sgamma = smask * gamma_chunk[row_head][:, None]
    return cos_t, sin_t, xi_t, zeta_t, decay, smask, sgamma


def _retention(qk, v, g):
    B, T, Wq2 = qk.shape
    Wv = v.shape[-1]
    C = min(RET_KERNEL_CHUNK, T)
    assert T % C == 0
    tables = _retention_tables(T, C)
    seq = lambda n: pl.BlockSpec((None, T, n), lambda b: (b, 0, 0))
    return pl.pallas_call(
        functools.partial(_retention_kernel, chunk=C, heads=RET_HEADS, dk=RET_KEY_DIM, dv=RET_VALUE_DIM),
        grid=(B,),
        in_specs=[seq(Wq2), seq(Wv), seq(Wv)] + [_resident(t.shape) for t in tables],
        out_specs=seq(Wv),
        out_shape=jax.ShapeDtypeStruct((B, T, Wv), BF16),
        scratch_shapes=[pltpu.VMEM((Wq2 // 2, Wv), F32)],
        compiler_params=_params(1),
        name="retention",
    )(qk, v, g, *tables)


def _pool_kernel(u_ref, w_ref, scale_ref, o_ref, buf_a, buf_b, *, windows):
    T, W = u_ref.shape
    P = POOL_PAD_ROWS
    gw = W // len(windows)
    u = u_ref[...]
    zeros = jnp.zeros((P, W), F32)
    buf_a[0:P, :] = zeros
    buf_b[0:P, :] = zeros
    buf_a[P:P + T, :] = u
    lane = lax.broadcasted_iota(jnp.int32, (1, W), 1)
    t1 = lax.broadcasted_iota(jnp.int32, (T, W), 0) + 1
    src, dst = buf_a, buf_b
    span = 1
    win = jnp.zeros((T, W), F32)
    cnt = jnp.ones((T, W), jnp.int32)
    for gi, w in enumerate(windows):
        while span < w:
            cur = src[P:P + T, :] + src[P - span:P - span + T, :]
            dst[P:P + T, :] = cur
            src, dst = dst, src
            span *= 2
        assert span == w
        in_group = (lane >= gi * gw) & (lane < (gi + 1) * gw)
        win = jnp.where(in_group, src[P:P + T, :], win)
        cnt = jnp.where(in_group, jnp.minimum(t1, w), cnt)
    mixed = win / cnt.astype(F32) - u
    o_ref[...] = (_dot(mixed.astype(BF16), w_ref[...]) * scale_ref[...]).astype(o_ref.dtype)


def _pool(u, w_blockdiag, scale):
    B, T, W = u.shape
    assert all(w & (w - 1) == 0 for w in POOL_WINDOWS) and max(POOL_WINDOWS) <= POOL_PAD_ROWS
    seq = pl.BlockSpec((None, T, W), lambda b: (b, 0, 0))
    return pl.pallas_call(
        functools.partial(_pool_kernel, windows=POOL_WINDOWS),
        grid=(B,),
        in_specs=[seq, _resident(w_blockdiag.shape), _resident(scale.shape)],
        out_specs=seq,
        out_shape=jax.ShapeDtypeStruct((B, T, W), BF16),
        scratch_shapes=[pltpu.VMEM((T + POOL_PAD_ROWS, W), F32)] * 2,
        compiler_params=_params(1),
        name="pool",
    )(u, w_blockdiag, scale)


def _merge_kernel(x_ref, attn_ref, ret_ref, pool_ref, gates_ref, wa_ref, wr_ref, wp_ref, wo_ref, o_ref):
    d = x_ref.shape[-1]
    merged = (gates_ref[:, 0:d].astype(F32) * _dot(attn_ref[...], wa_ref[...])
              + gates_ref[:, d:2 * d].astype(F32) * _dot(ret_ref[...], wr_ref[...])
              + gates_ref[:, 2 * d:3 * d].astype(F32) * _dot(pool_ref[...], wp_ref[...]))
    o_ref[...] = x_ref[...] + _dot(merged.astype(BF16), wo_ref[...])


def _merge(x, attn, ret, pooled, gates, wa, wr, wp, wo):
    B, T, D = x.shape
    tm = min(TOKEN_TILE, T)
    tile = lambda n: pl.BlockSpec((None, tm, n), lambda b, t: (b, t, 0))
    return pl.pallas_call(
        _merge_kernel,
        grid=(B, T // tm),
        in_specs=[tile(D), tile(attn.shape[-1]), tile(ret.shape[-1]), tile(pooled.shape[-1]), tile(3 * D),
                  _resident(wa.shape), _resident(wr.shape), _resident(wp.shape), _resident(wo.shape)],
        out_specs=tile(D),
        out_shape=jax.ShapeDtypeStruct((B, T, D), F32),
        input_output_aliases={0: 0},
        compiler_params=_params(2),
        name="merge",
    )(x, attn, ret, pooled, gates, wa, wr, wp, wo)


def _ffn_kernel(x_ref, p_ref, gf_ref, wu_ref, cw_ref, cb_ref, wd_ref, gp_ref, wg_ref, wpp_ref, gfin_ref,
                o_ref, carry_ref, pad_ref, *, ffn_dim, col_tile, final_norm):
    tm = x_ref.shape[0]
    HALO = CONV_HALO_ROWS
    n_col = ffn_dim // col_tile

    @pl.when(pl.program_id(1) == 0)
    def _():
        carry_ref[...] = jnp.zeros_like(carry_ref)

    x = x_ref[...]
    h = _rms(x, gf_ref[...]).astype(BF16)

    def conv_up(col, slot, buf):
        cs = slice(col, col + col_tile)
        up = _dot(h, wu_ref[:, cs])
        pad_ref[buf, 0:HALO, :] = carry_ref[slot]
        pad_ref[buf, HALO:HALO + tm, :] = up
        carry_ref[slot] = up[tm - HALO:tm, :]
        cw = cw_ref[:, cs]
        out = cw[CONV_WIDTH - 1:CONV_WIDTH, :] * up + cb_ref[:, cs]
        for tap in range(CONV_WIDTH - 1):
            back = CONV_WIDTH - 1 - tap
            out = out + cw[tap:tap + 1, :] * pad_ref[buf, HALO - back:HALO - back + tm, :]
        return out

    acc = jnp.zeros(x.shape, F32)
    for j in range(n_col):
        a = conv_up(j * col_tile, j, 0)
        b = conv_up(ffn_dim + j * col_tile, n_col + j, 1)
        act = (jax.nn.gelu(a, approximate=True) * b).astype(BF16)
        acc = acc + _dot(act, wd_ref[j * col_tile:(j + 1) * col_tile, :])
    x = x + acc
    h = _rms(x, gp_ref[...]).astype(BF16)
    x = x + jax.nn.sigmoid(_dot(h, wg_ref[...])) * _dot(p_ref[...].astype(BF16), wpp_ref[...])
    if final_norm:
        x = _rms(x, gfin_ref[...])
    o_ref[...] = x


def _ffn(x, p, gf, wu, cw, cb, wd, gp, wg, wpp, gfin, *, final_norm):
    B, T, D = x.shape
    ffn_dim = wd.shape[0]
    tm = min(TOKEN_TILE, T)
    assert ffn_dim % FFN_COL_TILE == 0 and tm >= CONV_HALO_ROWS >= CONV_WIDTH - 1
    n_slots = 2 * ffn_dim // FFN_COL_TILE
    tile = lambda n: pl.BlockSpec((None, tm, n), lambda b, t: (b, t, 0))
    weights = (gf, wu, cw, cb, wd, gp, wg, wpp, gfin)
    return pl.pallas_call(
        functools.partial(_ffn_kernel, ffn_dim=ffn_dim, col_tile=FFN_COL_TILE, final_norm=final_norm),
        grid=(B, T // tm),
        in_specs=[tile(D), tile(p.shape[-1])] + [_resident(w.shape) for w in weights],
        out_specs=tile(D),
        out_shape=jax.ShapeDtypeStruct((B, T, D), F32),
        scratch_shapes=[pltpu.VMEM((n_slots, CONV_HALO_ROWS, FFN_COL_TILE), F32),
                        pltpu.VMEM((2, CONV_HALO_ROWS + tm, FFN_COL_TILE), F32)],
        input_output_aliases={0: 0},
        compiler_params=_params(2),
        name="ffn",
    )(x, p, *weights)


def _block_diag(w):
    G, c, _ = w.shape
    eye = jnp.eye(G, dtype=w.dtype)
    return (eye[:, None, :, None] * w[:, :, None, :]).reshape(G * c, G * c)


def kernel(x, p, norm_mix_g, w_in, w_branch_attn, w_branch_ret, w_branch_pool, pool_w, pool_scale, w_out,
           norm_ffn_g, w_up, conv_w, conv_b, w_down, norm_ple_g, w_ple_gate, w_ple_proj, norm_final_g):
    depth = w_in.shape[0]
    row = lambda v: v.reshape(1, -1)
    for i in range(depth):
        qkva, qkr, vr, gr, u_pool, gates = _inproj(x, row(norm_mix_g[i]), w_in[i].astype(BF16))
        attn = _moba(qkva)
        ret = _retention(qkr, vr, gr)
        pooled = _pool(u_pool, _block_diag(pool_w[i]).astype(BF16), row(pool_scale[i]))
        x = _merge(x, attn, ret, pooled, gates,
                   w_branch_attn[i].astype(BF16), w_branch_ret[i].astype(BF16),
                   w_branch_pool[i].astype(BF16), w_out[i].astype(BF16))
        x = _ffn(x, p[i], row(norm_ffn_g[i]), w_up[i].astype(BF16), conv_w[i], row(conv_b[i]),
                 w_down[i].astype(BF16), row(norm_ple_g[i]), w_ple_gate[i].astype(BF16),
                 w_ple_proj[i].astype(BF16), row(norm_final_g), final_norm=(i == depth - 1))
    return x
```

```python
import functools

import jax
import jax.numpy as jnp
import numpy as np
from jax import lax
from jax.experimental import pallas as pl
from jax.experimental.pallas import tpu as pltpu

F32 = jnp.float32
BF16 = jnp.bfloat16

NORM_EPS = 1e-6

ATTN_HEADS = 4
ATTN_HEAD_DIM = 64
ATTN_WIDTH = ATTN_HEADS * ATTN_HEAD_DIM
MOBA_BLOCK = 256
MOBA_TOPK = 3
MOBA_GATE_ROWS = 8
MOBA_MASKED = -1e30

RET_HEADS = 4
RET_KEY_DIM = 64
RET_VALUE_DIM = 128
RET_QK_WIDTH = RET_HEADS * RET_KEY_DIM
RET_V_WIDTH = RET_HEADS * RET_VALUE_DIM
ROPE_BASE = 10000.0
RET_KERNEL_CHUNK = 256

POOL_WINDOWS = (2, 4, 8, 16)
POOL_GROUPS = len(POOL_WINDOWS)
POOL_PAD_ROWS = 16

CONV_WIDTH = 3
CONV_HALO_ROWS = 8

LANE_TILE = 128
TOKEN_TILE = 512
FFN_COL_TILE = 256
VMEM_LIMIT_BYTES = 56 * 1024 * 1024


def _rms(x, g):
    return x * lax.rsqrt(jnp.mean(x * x, axis=-1, keepdims=True) + NORM_EPS) * g


def _dot(a, b):
    return jnp.dot(a, b, preferred_element_type=F32)


def _dot_nt(a, b):
    return lax.dot_general(a, b, (((1,), (1,)), ((), ())), preferred_element_type=F32)


def _resident(shape):
    nd = len(shape)
    return pl.BlockSpec(shape, lambda *_: (0,) * nd, pipeline_mode=pl.Buffered(1))


def _layer(stacked, i):
    nd = stacked.ndim - 1
    return pl.BlockSpec((None,) + stacked.shape[1:], lambda *_: (i,) + (0,) * nd, pipeline_mode=pl.Buffered(1))


def _params(n_axes):
    return pltpu.CompilerParams(dimension_semantics=("arbitrary",) * n_axes,
                                vmem_limit_bytes=VMEM_LIMIT_BYTES)


def _inproj_kernel(x_ref, g_ref, w_ref, qkva_ref, qkr_ref, vr_ref, gr_ref, up_ref, gates_ref, *, splits):
    h = _rms(x_ref[...], g_ref[...]).astype(BF16)

    def proj(lo, hi):
        return _dot(h, w_ref[:, lo:hi])

    a0, r0, v0, g0, p0, s0, end = splits
    qkva_ref[...] = proj(a0, r0).astype(BF16)
    qkr_ref[...] = proj(r0, v0)
    vr_ref[...] = proj(v0, g0).astype(BF16)
    gr_ref[...] = proj(g0, p0)
    up_ref[...] = proj(p0, s0)
    d = x_ref.shape[-1]
    for c in range((end - s0) // d):
        gates_ref[:, c * d:(c + 1) * d] = jax.nn.sigmoid(proj(s0 + c * d, s0 + (c + 1) * d)).astype(BF16)


def _inproj(x, g, w, i):
    B, T, D = x.shape
    tm = min(TOKEN_TILE, T)
    aw, rq, rv, pw = 3 * ATTN_WIDTH, 2 * RET_QK_WIDTH, RET_V_WIDTH, D // 4
    splits = tuple(int(s) for s in np.cumsum([0, aw, rq, rv, rv, pw, 3 * D]))
    assert splits[-1] == w.shape[-1]
    widths = (aw, rq, rv, rv, pw, 3 * D)
    dtypes = (BF16, F32, BF16, F32, F32, BF16)
    tile = lambda n: pl.BlockSpec((None, tm, n), lambda b, t: (b, t, 0))
    return pl.pallas_call(
        functools.partial(_inproj_kernel, splits=splits),
        grid=(B, T // tm),
        in_specs=[tile(D), _layer(g, i), _layer(w, i)],
        out_specs=[tile(n) for n in widths],
        out_shape=[jax.ShapeDtypeStruct((B, T, n), dt) for n, dt in zip(widths, dtypes)],
        compiler_params=_params(2),
        name="inproj",
    )(x, g, w)


def _moba_kernel(qkv_ref, o_ref, *, block, heads, head_dim, top_k):
    T = qkv_ref.shape[0]
    W = heads * head_dim
    L = block
    n_blk = T // L
    G = MOBA_GATE_ROWS
    scale = head_dim ** -0.5
    lane = lax.broadcasted_iota(jnp.int32, (1, W), 1)
    head_masks = [(lane >= h * head_dim) & (lane < (h + 1) * head_dim) for h in range(heads)]

    k_all = qkv_ref[:, W:2 * W]
    v_all = qkv_ref[:, 2 * W:3 * W]
    v_masked = [jnp.where(m, v_all, jnp.zeros_like(v_all)) for m in head_masks]

    rows = lax.broadcasted_iota(jnp.int32, (LANE_TILE, T), 0)
    cols = lax.broadcasted_iota(jnp.int32, (LANE_TILE, T), 1)
    avg = jnp.where((cols >= rows * L) & (cols < (rows + 1) * L), 1.0 / L, 0.0).astype(BF16)
    k_mean = _dot(avg, k_all)[0:G, :]
    km = jnp.concatenate([jnp.where(m, k_mean, 0.0) for m in head_masks]
                         + [jnp.zeros((LANE_TILE - heads * G, W), F32)], axis=0)
    km_hi = km.astype(BF16)
    km_lo = (km - km_hi.astype(F32)).astype(BF16)

    key_blk = lax.broadcasted_iota(jnp.int32, (T, LANE_TILE), 0) // L
    ind_lane = lax.broadcasted_iota(jnp.int32, (T, LANE_TILE), 1)
    k_aug = []
    for h in range(heads):
        ind = jnp.where(ind_lane == key_blk + G * h, 1.0, 0.0).astype(BF16)
        lower = (h + 1) * head_dim <= LANE_TILE
        k_aug.append(jnp.concatenate([k_all[:, :LANE_TILE], ind] if lower else [ind, k_all[:, LANE_TILE:]], axis=1))

    blk_row = lax.broadcasted_iota(jnp.int32, (G, L), 0)
    q_pos = lax.broadcasted_iota(jnp.int32, (L, L), 0)
    k_pos = lax.broadcasted_iota(jnp.int32, (L, L), 1)
    causal = k_pos <= q_pos
    pow2_scale = float(np.log2(scale)).is_integer()

    for i in range(n_blk):
        q_i = qkv_ref[i * L:(i + 1) * L, 0:W]
        if i > top_k:
            gates = _dot_nt(km_hi, q_i) + _dot_nt(km_lo, q_i)
            biases = []
            for h in range(heads):
                g = gates[G * h:G * (h + 1), :]
                rank = jnp.zeros((G, L), jnp.int32)
                for j in range(i):
                    g_j = g[j:j + 1, :]
                    rank = rank + ((g_j > g) | ((g_j == g) & (j < blk_row))).astype(jnp.int32)
                biases.append(jnp.where((rank < top_k) | (blk_row >= i), 0.0, MOBA_MASKED))
            biases.append(jnp.zeros((LANE_TILE - heads * G, L), F32))
            bias = jnp.concatenate(biases, axis=0).T.astype(BF16)
        else:
            bias = jnp.zeros((L, LANE_TILE), BF16)
        if pow2_scale:
            q_i = q_i * jnp.asarray(scale, q_i.dtype)
        acc = jnp.zeros((L, W), F32)
        for h in range(heads):
            q_h = jnp.where(head_masks[h], q_i, jnp.zeros_like(q_i))
            lower = (h + 1) * head_dim <= LANE_TILE
            q_aug = jnp.concatenate([q_h[:, :LANE_TILE], bias] if lower else [bias, q_h[:, LANE_TILE:]], axis=1)
            s = _dot_nt(q_aug, k_aug[h][0:(i + 1) * L, :])
            if not pow2_scale:
                s = s * scale
            own = jnp.where(causal, s[:, i * L:(i + 1) * L], -jnp.inf)
            s = jnp.concatenate([s[:, 0:i * L], own], axis=1) if i else own
            m = jnp.max(s, axis=-1, keepdims=True)
            p = jnp.exp(s - m)
            inv_l = 1.0 / jnp.sum(p, axis=-1, keepdims=True)
            acc = acc + _dot(p.astype(BF16), v_masked[h][0:(i + 1) * L, :]) * inv_l
        o_ref[i * L:(i + 1) * L, :] = acc.astype(o_ref.dtype)


def _moba(qkv):
    B, T, W3 = qkv.shape
    W = W3 // 3
    assert T % MOBA_BLOCK == 0 and T // MOBA_BLOCK <= MOBA_GATE_ROWS
    assert W == 2 * LANE_TILE and LANE_TILE % ATTN_HEAD_DIM == 0 and ATTN_HEADS * MOBA_GATE_ROWS <= LANE_TILE
    return pl.pallas_call(
        functools.partial(_moba_kernel, block=MOBA_BLOCK, heads=ATTN_HEADS,
                          head_dim=ATTN_HEAD_DIM, top_k=MOBA_TOPK),
        grid=(B,),
        in_specs=[pl.BlockSpec((None, T, W3), lambda b: (b, 0, 0))],
        out_specs=pl.BlockSpec((None, T, W), lambda b: (b, 0, 0)),
        out_shape=jax.ShapeDtypeStruct((B, T, W), BF16),
        compiler_params=_params(1),
        name="moba",
    )(qkv)


def _retention_kernel(qk_ref, v_ref, g_ref, cos_ref, sin_ref, xi_ref, zeta_ref, decay_ref,
                      smask_ref, sgamma_ref, o_ref, state_ref, *, chunk, heads, dk, dv):
    T = qk_ref.shape[0]
    C = chunk
    Wq = heads * dk
    half = dk // 2
    lane = lax.broadcasted_iota(jnp.int32, (1, Wq), 1)
    first_half = (lane % dk) < half
    head_masks = [(lane >= h * dk) & (lane < (h + 1) * dk) for h in range(heads)]

    def rotate(x, c, s):
        partner = jnp.where(first_half, pltpu.roll(x, Wq - half, 1), pltpu.roll(x, half, 1))
        return x * c + partner * s

    state_ref[...] = jnp.zeros_like(state_ref)

    def body(n, carry):
        r = pl.ds(pl.multiple_of(n * C, C), C)
        c, s = cos_ref[r, :], sin_ref[r, :]
        q = rotate(qk_ref[r, 0:Wq], c, s)
        k = rotate(qk_ref[r, Wq:2 * Wq], c, s) * (dk ** -0.5)
        v = v_ref[r, :]
        k_b = k.astype(BF16)
        state = state_ref[...]
        cross = _dot((q * xi_ref[...]).astype(BF16), state.astype(BF16))
        outs = []
        for h in range(heads):
            q_h = jnp.where(head_masks[h], q, 0.0).astype(BF16)
            scores = _dot_nt(q_h, k_b) * decay_ref[h]
            y = _dot(scores.astype(BF16), v[:, h * dv:(h + 1) * dv]) + cross[:, h * dv:(h + 1) * dv]
            mu = jnp.mean(y, axis=-1, keepdims=True)
            yc = y - mu
            var = jnp.mean(yc * yc, axis=-1, keepdims=True)
            outs.append(yc * lax.rsqrt(var + NORM_EPS))
        y = jnp.concatenate(outs, axis=1)
        o_ref[r, :] = (jax.nn.silu(g_ref[r, :]) * y).astype(o_ref.dtype)
        u = _dot((k * zeta_ref[...]).T.astype(BF16), v)
        state_ref[...] = u * smask_ref[...] + state * sgamma_ref[...]
        return carry

    lax.fori_loop(0, T // C, body, 0)


def _retention_tables(T, C):
    H, dk, dv = RET_HEADS, RET_KEY_DIM, RET_VALUE_DIM
    half = dk // 2
    pos = jnp.arange(T, dtype=F32)
    inv_freq = ROPE_BASE ** (-jnp.arange(half, dtype=F32) / half)
    ang = pos[:, None] * inv_freq[None, :]
    cos, sin = jnp.cos(ang), jnp.sin(ang)
    cos_t = jnp.tile(jnp.concatenate([cos, cos], axis=1), (1, H))
    sin_t = jnp.tile(jnp.concatenate([-sin, sin], axis=1), (1, H))
    log_gamma = jnp.log1p(-jnp.exp2(-5.0 - jnp.arange(H, dtype=F32)))
    idx = jnp.arange(C, dtype=F32)
    rel = idx[:, None] - idx[None, :]
    decay = jnp.where(rel[None] >= 0, jnp.exp(jnp.maximum(rel, 0.0)[None] * log_gamma[:, None, None]), 0.0)
    zeta = jnp.exp((C - 1.0 - idx)[None, :] * log_gamma[:, None])
    xi = jnp.exp((idx + 1.0)[None, :] * log_gamma[:, None])
    gamma_chunk = jnp.exp(C * log_gamma)
    xi_t = jnp.repeat(xi.T, dk, axis=1)
    zeta_t = jnp.repeat(zeta.T, dk, axis=1)
    row_head = jnp.arange(H * dk) // dk
    col_head = jnp.arange(H * dv) // dv
    smask = (row_head[:, None] == col_head[None, :]).astype(F32)
    sgamma = smask * gamma_chunk[row_head][:, None]
    return cos_t, sin_t, xi_t, zeta_t, decay, smask, sgamma


def _retention(qk, v, g):
    B, T, Wq2 = qk.shape
    Wv = v.shape[-1]
    C = min(RET_KERNEL_CHUNK, T)
    assert T % C == 0
    tables = _retention_tables(T, C)
    seq = lambda n: pl.BlockSpec((None, T, n), lambda b: (b, 0, 0))
    return pl.pallas_call(
        functools.partial(_retention_kernel, chunk=C, heads=RET_HEADS, dk=RET_KEY_DIM, dv=RET_VALUE_DIM),
        grid=(B,),
        in_specs=[seq(Wq2), seq(Wv), seq(Wv)] + [_resident(t.shape) for t in tables],
        out_specs=seq(Wv),
        out_shape=jax.ShapeDtypeStruct((B, T, Wv), BF16),
        scratch_shapes=[pltpu.VMEM((Wq2 // 2, Wv), F32)],
        compiler_params=_params(1),
        name="retention",
    )(qk, v, g, *tables)


def _pool_kernel(u_ref, w_ref, scale_ref, o_ref, buf_a, buf_b, *, windows):
    T, W = u_ref.shape
    P = POOL_PAD_ROWS
    gw = W // len(windows)
    u = u_ref[...]
    zeros = jnp.zeros((P, W), F32)
    buf_a[0:P, :] = zeros
    buf_b[0:P, :] = zeros
    buf_a[P:P + T, :] = u
    lane = lax.broadcasted_iota(jnp.int32, (1, W), 1)
    t1 = lax.broadcasted_iota(jnp.int32, (T, W), 0) + 1
    src, dst = buf_a, buf_b
    span = 1
    win = jnp.zeros((T, W), F32)
    cnt = jnp.ones((T, W), jnp.int32)
    for gi, w in enumerate(windows):
        while span < w:
            cur = src[P:P + T, :] + src[P - span:P - span + T, :]
            dst[P:P + T, :] = cur
            src, dst = dst, src
            span *= 2
        assert span == w
        in_group = (lane >= gi * gw) & (lane < (gi + 1) * gw)
        win = jnp.where(in_group, src[P:P + T, :], win)
        cnt = jnp.where(in_group, jnp.minimum(t1, w), cnt)
    mixed = win / cnt.astype(F32) - u
    o_ref[...] = (_dot(mixed.astype(BF16), w_ref[...]) * scale_ref[...]).astype(o_ref.dtype)


def _pool(u, w_blockdiag, scale, i):
    B, T, W = u.shape
    assert all(w & (w - 1) == 0 for w in POOL_WINDOWS) and max(POOL_WINDOWS) <= POOL_PAD_ROWS
    seq = pl.BlockSpec((None, T, W), lambda b: (b, 0, 0))
    return pl.pallas_call(
        functools.partial(_pool_kernel, windows=POOL_WINDOWS),
        grid=(B,),
        in_specs=[seq, _layer(w_blockdiag, i), _layer(scale, i)],
        out_specs=seq,
        out_shape=jax.ShapeDtypeStruct((B, T, W), BF16),
        scratch_shapes=[pltpu.VMEM((T + POOL_PAD_ROWS, W), F32)] * 2,
        compiler_params=_params(1),
        name="pool",
    )(u, w_blockdiag, scale)


def _merge_kernel(x_ref, attn_ref, ret_ref, pool_ref, gates_ref, wa_ref, wr_ref, wp_ref, wo_ref, o_ref):
    d = x_ref.shape[-1]
    merged = (gates_ref[:, 0:d].astype(F32) * _dot(attn_ref[...], wa_ref[...])
              + gates_ref[:, d:2 * d].astype(F32) * _dot(ret_ref[...], wr_ref[...])
              + gates_ref[:, 2 * d:3 * d].astype(F32) * _dot(pool_ref[...], wp_ref[...]))
    o_ref[...] = x_ref[...] + _dot(merged.astype(BF16), wo_ref[...])


def _merge(x, attn, ret, pooled, gates, wa, wr, wp, wo, i):
    B, T, D = x.shape
    tm = min(TOKEN_TILE, T)
    tile = lambda n: pl.BlockSpec((None, tm, n), lambda b, t: (b, t, 0))
    return pl.pallas_call(
        _merge_kernel,
        grid=(B, T // tm),
        in_specs=[tile(D), tile(attn.shape[-1]), tile(ret.shape[-1]), tile(pooled.shape[-1]), tile(3 * D),
                  _layer(wa, i), _layer(wr, i), _layer(wp, i), _layer(wo, i)],
        out_specs=tile(D),
        out_shape=jax.ShapeDtypeStruct((B, T, D), F32),
        compiler_params=_params(2),
        name="merge",
    )(x, attn, ret, pooled, gates, wa, wr, wp, wo)


def _ffn_kernel(x_ref, p_ref, gf_ref, wu_ref, cw_ref, cb_ref, wd_ref, gp_ref, wg_ref, wpp_ref, gfin_ref,
                o_ref, carry_ref, *, ffn_dim, col_tile, final_norm):
    tm = x_ref.shape[0]
    HALO = CONV_HALO_ROWS
    n_col = ffn_dim // col_tile

    @pl.when(pl.program_id(1) == 0)
    def _():
        carry_ref[...] = jnp.zeros_like(carry_ref)

    x = x_ref[...]
    h = _rms(x, gf_ref[...]).astype(BF16)

    halo_row = lax.broadcasted_iota(jnp.int32, (HALO, col_tile), 0)

    def up_proj(col):
        return _dot(h, wu_ref[:, col:col + col_tile])

    def conv(up, col, slot):
        cs = slice(col, col + col_tile)
        prev = carry_ref[slot]
        carry_ref[slot] = up[tm - HALO:tm, :]
        cw = cw_ref[:, cs]
        out = cw[CONV_WIDTH - 1:CONV_WIDTH, :] * up + cb_ref[:, cs]
        for tap in range(CONV_WIDTH - 1):
            back = CONV_WIDTH - 1 - tap
            rolled = pltpu.roll(up, back, 0)
            head = jnp.where(halo_row < back, pltpu.roll(prev, back, 0), rolled[0:HALO, :])
            shifted = jnp.concatenate([head, rolled[HALO:, :]], axis=0)
            out = out + cw[tap:tap + 1, :] * shifted
        return out

    acc = jnp.zeros(x.shape, F32)
    ups = (up_proj(0), up_proj(ffn_dim))
    for j in range(n_col):
        up_a, up_b = ups
        if j + 1 < n_col:
            ups = (up_proj((j + 1) * col_tile), up_proj(ffn_dim + (j + 1) * col_tile))
        a = conv(up_a, j * col_tile, j)
        b = conv(up_b, ffn_dim + j * col_tile, n_col + j)
        act = (jax.nn.gelu(a, approximate=True) * b).astype(BF16)
        acc = acc + _dot(act, wd_ref[j * col_tile:(j + 1) * col_tile, :])
    x = x + acc
    h = _rms(x, gp_ref[...]).astype(BF16)
    x = x + jax.nn.sigmoid(_dot(h, wg_ref[...])) * _dot(p_ref[...].astype(BF16), wpp_ref[...])
    if final_norm:
        x = _rms(x, gfin_ref[...])
    o_ref[...] = x


def _ffn(x, p, gf, wu, cw, cb, wd, gp, wg, wpp, gfin, i, *, final_norm):
    B, T, D = x.shape
    ffn_dim = wd.shape[-2]
    tm = min(TOKEN_TILE, T)
    assert ffn_dim % FFN_COL_TILE == 0 and tm >= CONV_HALO_ROWS >= CONV_WIDTH - 1
    n_slots = 2 * ffn_dim // FFN_COL_TILE
    tile = lambda n: pl.BlockSpec((None, tm, n), lambda b, t: (b, t, 0))
    p_tile = pl.BlockSpec((None, None, tm, p.shape[-1]), lambda b, t: (i, b, t, 0))
    weights = (gf, wu, cw, cb, wd, gp, wg, wpp)
    return pl.pallas_call(
        functools.partial(_ffn_kernel, ffn_dim=ffn_dim, col_tile=FFN_COL_TILE, final_norm=final_norm),
        grid=(B, T // tm),
        in_specs=[tile(D), p_tile] + [_layer(w, i) for w in weights] + [_resident(gfin.shape)],
        out_specs=tile(D),
        out_shape=jax.ShapeDtypeStruct((B, T, D), F32),
        scratch_shapes=[pltpu.VMEM((n_slots, CONV_HALO_ROWS, FFN_COL_TILE), F32)],
        compiler_params=_params(2),
        name="ffn",
    )(x, p, *weights, gfin)


def _block_diag(w):
    depth, G, c, _ = w.shape
    eye = jnp.eye(G, dtype=w.dtype)
    return (eye[None, :, None, :, None] * w[:, :, :, None, :]).reshape(depth, G * c, G * c)


def kernel(x, p, norm_mix_g, w_in, w_branch_attn, w_branch_ret, w_branch_pool, pool_w, pool_scale, w_out,
           norm_ffn_g, w_up, conv_w, conv_b, w_down, norm_ple_g, w_ple_gate, w_ple_proj, norm_final_g):
    depth = w_in.shape[0]
    rows = lambda v: v.reshape(depth, 1, -1)
    bf = lambda w: w.astype(BF16)
    g_mix, g_ffn, g_ple = rows(norm_mix_g), rows(norm_ffn_g), rows(norm_ple_g)
    w_in, w_up, w_down, w_out = bf(w_in), bf(w_up), bf(w_down), bf(w_out)
    wb_attn, wb_ret, wb_pool = bf(w_branch_attn), bf(w_branch_ret), bf(w_branch_pool)
    w_ple_gate, w_ple_proj = bf(w_ple_gate), bf(w_ple_proj)
    pool_bd, pool_sc, conv_b = bf(_block_diag(pool_w)), rows(pool_scale), rows(conv_b)
    g_final = norm_final_g.reshape(1, -1)
    for i in range(depth):
        qkva, qkr, vr, gr, u_pool, gates = _inproj(x, g_mix, w_in, i)
        attn = _moba(qkva)
        ret = _retention(qkr, vr, gr)
        pooled = _pool(u_pool, pool_bd, pool_sc, i)
        x = _merge(x, attn, ret, pooled, gates, wb_attn, wb_ret, wb_pool, w_out, i)
        x = _ffn(x, p, g_ffn, w_up, conv_w, conv_b, w_down, g_ple, w_ple_gate, w_ple_proj, g_final, i,
                 final_norm=(i == depth - 1))
    return x
```
